```python
import jax, jax.numpy as jnp
from jax import lax
import numpy as np

D_MODEL = 1024
BATCH = 8
SEQ = 2048
DEPTH = 1

GRID_W = 64
CTX_LEN = 256
HEAD_DIM = 64
A_HEADS = 8
A_KV_HEADS = 2
A_WINDOW = 128
A_BLOCK = 128
B_HEADS = 8
B_WIN_ROWS = 8
B_WIN_COLS = 16
N_EXPERTS = 32
TOP_K = 4
D_EXPERT = D_MODEL
SWIGLU_LIMIT = 7.0
SWIGLU_ALPHA = 1.702
ROPE_THETA = 10000.0
NORM_EPS = 1e-6
MOE_BLOCK = 128
A_Q = A_HEADS * HEAD_DIM
A_KV = A_KV_HEADS * HEAD_DIM
B_W = B_HEADS * HEAD_DIM
IN_COLS = A_Q + 2 * A_KV + 3 * B_W + 2 * D_MODEL

kernel_name = "hybrid_dit_window_natten_moe"


def rms_norm(x, g):
    xf = x.astype(jnp.float32)
    y = xf * lax.rsqrt(jnp.mean(xf * xf, axis=-1, keepdims=True) + NORM_EPS)
    return (y * g.astype(jnp.float32)).astype(x.dtype)


def modulate(h, shift, scale):
    return h * (1 + scale) + shift


def softmax_with_sink(s, sink):
    if sink is None:
        return jax.nn.softmax(s, axis=-1)
    sink = sink.astype(jnp.float32)
    m = jnp.maximum(jnp.max(s, axis=-1, keepdims=True), sink)
    e = jnp.exp(s - m)
    return e / (jnp.sum(e, axis=-1, keepdims=True) + jnp.exp(sink - m))


def axial_rope_tables(n_tokens):
    t = jnp.arange(n_tokens, dtype=jnp.int32)
    row = (t // GRID_W).astype(jnp.float32)
    col = (t % GRID_W).astype(jnp.float32)
    n_freq = HEAD_DIM // 4
    inv_freq = ROPE_THETA ** (-jnp.arange(n_freq, dtype=jnp.float32) / n_freq)
    ang = jnp.concatenate([row[:, None] * inv_freq, col[:, None] * inv_freq], axis=-1)
    return jnp.cos(ang), jnp.sin(ang)


def apply_axial_rope(x, cos, sin):
    b, s, h, d = x.shape
    nf = HEAD_DIM // 4
    xr = x.reshape(b, s, h, 2, 2, nf)
    x1, x2 = xr[..., 0, :], xr[..., 1, :]
    c = cos.reshape(s, 1, 2, nf).astype(x.dtype)
    sn = sin.reshape(s, 1, 2, nf).astype(x.dtype)
    out = jnp.stack([x1 * c - x2 * sn, x2 * c + x1 * sn], axis=-2)
    return out.reshape(b, s, h, d)


def split_projection(p):
    bounds = [A_Q, A_Q + A_KV, A_Q + 2 * A_KV, A_Q + 2 * A_KV + B_W,
              A_Q + 2 * A_KV + 2 * B_W, A_Q + 2 * A_KV + 3 * B_W,
              A_Q + 2 * A_KV + 3 * B_W + D_MODEL]
    qa, ka, va, qb, kb, vb, ga, gb = jnp.split(p, bounds, axis=-1)
    heads = lambda t, n: t.reshape(*t.shape[:-1], n, HEAD_DIM)
    return (heads(qa, A_HEADS), heads(ka, A_KV_HEADS), heads(va, A_KV_HEADS),
            heads(qb, B_HEADS), heads(kb, B_HEADS), heads(vb, B_HEADS), ga, gb)


def windowed_sink_attention(q, k, v, k_ctx, v_ctx, sink):
    b, s, _, hd = q.shape
    nb = s // A_BLOCK
    g = A_HEADS // A_KV_HEADS
    scale = HEAD_DIM ** -0.5
    qb = q.reshape(b, nb, A_BLOCK, A_KV_HEADS, g, hd)
    pad = ((0, 0), (A_BLOCK, A_BLOCK), (0, 0), (0, 0))
    kp = jnp.pad(k, pad).reshape(b, nb + 2, A_BLOCK, A_KV_HEADS, hd)
    vp = jnp.pad(v, pad).reshape(b, nb + 2, A_BLOCK, A_KV_HEADS, hd)
    kb = jnp.concatenate([kp[:, :-2], kp[:, 1:-1], kp[:, 2:]], axis=2)
    vb = jnp.concatenate([vp[:, :-2], vp[:, 1:-1], vp[:, 2:]], axis=2)
    s_loc = jnp.einsum('bnqkgd,bnjkd->bnkgqj', qb, kb).astype(jnp.float32) * scale
    qpos = jnp.arange(nb)[:, None] * A_BLOCK + jnp.arange(A_BLOCK)[None, :]
    kpos = jnp.arange(nb)[:, None] * A_BLOCK - A_BLOCK + jnp.arange(3 * A_BLOCK)[None, :]
    valid = ((jnp.abs(qpos[:, :, None] - kpos[:, None, :]) <= A_WINDOW)
             & (kpos >= 0)[:, None, :] & (kpos < s)[:, None, :])
    s_loc = jnp.where(valid[None, :, None, None], s_loc, -jnp.inf)
    s_ctx = jnp.einsum('bnqkgd,bckd->bnkgqc', qb, k_ctx).astype(jnp.float32) * scale
    n_loc = s_loc.shape[-1]
    p = softmax_with_sink(jnp.concatenate([s_loc, s_ctx], axis=-1),
                          sink.reshape(A_KV_HEADS, g)[None, None, :, :, None, None])
    p = p.astype(v.dtype)
    o = (jnp.einsum('bnkgqj,bnjkd->bnqkgd', p[..., :n_loc], vb)
         + jnp.einsum('bnkgqc,bckd->bnqkgd', p[..., n_loc:], v_ctx))
    return o.reshape(b, s, A_Q)


def neighbourhood_attention(q, k, v, k_ctx, v_ctx, rpb):
    b, s, h, hd = q.shape
    rows = s // GRID_W
    kr = min(B_WIN_ROWS, rows)
    kc = B_WIN_COLS
    scale = HEAD_DIM ** -0.5
    r = jnp.arange(rows)
    cidx = jnp.arange(GRID_W)
    row_start = jnp.clip(r - kr // 2, 0, rows - kr)
    row_idx = row_start[:, None] + jnp.arange(kr)[None, :]
    col_start = jnp.clip(cidx - kc // 2, 0, GRID_W - kc)
    col_in = (cidx[None, :] >= col_start[:, None]) & (cidx[None, :] < col_start[:, None] + kc)
    qg = q.reshape(b, rows, GRID_W, h, hd)
    kg = k.reshape(b, rows, GRID_W, h, hd)[:, row_idx]
    vg = v.reshape(b, rows, GRID_W, h, hd)[:, row_idx]
    s_loc = jnp.einsum('brqhd,brkwhd->bhrqkw', qg, kg).astype(jnp.float32) * scale
    roff = row_idx - r[:, None] + (B_WIN_ROWS - 1)
    coff = jnp.clip(cidx[None, :] - cidx[:, None], -(kc - 1), kc - 1) + (B_WIN_COLS - 1)
    bias = rpb[:, roff[:, None, :, None], coff[None, :, None, :]]
    s_loc = jnp.where(col_in[None, None, None, :, None, :], s_loc + bias[None].astype(jnp.float32), -jnp.inf)
    s_loc = s_loc.reshape(b, h, rows, GRID_W, kr * GRID_W)
    s_ctx = jnp.einsum('brqhd,bchd->bhrqc', qg, k_ctx).astype(jnp.float32) * scale
    n_loc = kr * GRID_W
    p = jax.nn.softmax(jnp.concatenate([s_loc, s_ctx], axis=-1), axis=-1).astype(v.dtype)
    p_loc = p[..., :n_loc].reshape(b, h, rows, GRID_W, kr, GRID_W)
    o = (jnp.einsum('bhrqkw,brkwhd->brqhd', p_loc, vg)
         + jnp.einsum('bhrqc,bchd->brqhd', p[..., n_loc:], v_ctx))
    return o.reshape(b, s, B_W)


def dense_context_attention(q, k, v, sink):
    b, c, h, hd = q.shape
    n_kv = k.shape[2]
    g = h // n_kv
    qg = q.reshape(b, c, n_kv, g, hd)
    s = jnp.einsum('bqkgd,bckd->bkgqc', qg, k).astype(jnp.float32) * (HEAD_DIM ** -0.5)
    snk = None if sink is None else sink.reshape(n_kv, g)[None, :, :, None, None]
    p = softmax_with_sink(s, snk).astype(v.dtype)
    o = jnp.einsum('bkgqc,bckd->bqkgd', p, v)
    return o.reshape(b, c, h * hd)


def merge_branches(ya, yb, ga, gb, w_branch_a, w_branch_b, w_out):
    merged = jax.nn.sigmoid(ga) * (ya @ w_branch_a) + jax.nn.sigmoid(gb) * (yb @ w_branch_b)
    return merged @ w_out


def moe_ffn(h, w_router, b_router, w_gate_up, b_gate_up, w_down, b_down):
    shp = h.shape
    xt = h.reshape(-1, D_MODEL)
    t = xt.shape[0]
    logits = (xt @ w_router + b_router).astype(jnp.float32)
    top_val, top_idx = lax.top_k(logits, TOP_K)
    weights = jax.nn.softmax(top_val, axis=-1)
    n_assign = t * TOP_K
    expert = top_idx.reshape(-1)
    token = jnp.arange(n_assign, dtype=jnp.int32) // TOP_K
    order = jnp.argsort(expert)
    sorted_expert = expert[order]
    sorted_token = token[order]
    counts = jnp.bincount(expert, length=N_EXPERTS)
    padded = ((counts + MOE_BLOCK - 1) // MOE_BLOCK) * MOE_BLOCK
    start = jnp.cumsum(counts) - counts
    pend = jnp.cumsum(padded)
    pstart = pend - padded
    dest = pstart[sorted_expert] + (jnp.arange(n_assign) - start[sorted_expert])
    n_blocks = -(-n_assign // MOE_BLOCK) + N_EXPERTS
    x_pad = jnp.zeros((n_blocks * MOE_BLOCK, D_MODEL), xt.dtype).at[dest].set(xt[sorted_token])
    block_expert = jnp.clip(jnp.searchsorted(pend, jnp.arange(n_blocks) * MOE_BLOCK, side='right'),
                            0, N_EXPERTS - 1)

    def expert_block(args):
        xb, e = args
        gu = xb @ w_gate_up[e] + b_gate_up[e]
        gate, up = jnp.split(gu, 2, axis=-1)
        gate = jnp.minimum(gate, SWIGLU_LIMIT)
        up = jnp.clip(up, -SWIGLU_LIMIT, SWIGLU_LIMIT)
        glu = gate * jax.nn.sigmoid(SWIGLU_ALPHA * gate)
        return ((up + 1) * glu) @ w_down[e] + b_down[e]

    y_pad = lax.map(expert_block, (x_pad.reshape(n_blocks, MOE_BLOCK, D_MODEL), block_expert))
    y_sorted = y_pad.reshape(-1, D_MODEL)[dest] * weights.reshape(-1)[order][:, None].astype(xt.dtype)
    y = jax.ops.segment_sum(y_sorted, sorted_token, num_segments=t)
    return y.reshape(shp)


def setup_inputs(seed: int = 0) -> dict:
    key = jax.random.key(seed)
    ks = jax.random.split(key, 24)
    f = jnp.float32
    L = DEPTH

    def nrm(k, shape, scale):
        return jax.random.normal(k, shape, f) * scale

    return {
        "x": nrm(ks[0], (BATCH, SEQ, D_MODEL), 1.0),
        "c": nrm(ks[1], (BATCH, D_MODEL), 1.0),
        "ctx": nrm(ks[2], (BATCH, CTX_LEN, D_MODEL), 1.0),
        "c_ctx": nrm(ks[3], (D_MODEL,), 1.0),
        "w_mod": nrm(ks[4], (L, D_MODEL, 6 * D_MODEL), 0.3 * D_MODEL ** -0.5),
        "b_mod": nrm(ks[5], (L, 6 * D_MODEL), 0.02),
        "g_pre_mix": 1.0 + nrm(ks[6], (L, D_MODEL), 0.05),
        "g_post_mix": 1.0 + nrm(ks[7], (L, D_MODEL), 0.05),
        "g_pre_ffn": 1.0 + nrm(ks[8], (L, D_MODEL), 0.05),
        "g_post_ffn": 1.0 + nrm(ks[9], (L, D_MODEL), 0.05),
        "w_in": nrm(ks[10], (L, D_MODEL, IN_COLS), D_MODEL ** -0.5),
        "a_sink": nrm(ks[11], (L, A_HEADS), 0.5),
        "b_rpb": nrm(ks[12], (L, B_HEADS, 2 * B_WIN_ROWS - 1, 2 * B_WIN_COLS - 1), 0.2),
        "w_branch_a": nrm(ks[13], (L, A_Q, D_MODEL), A_Q ** -0.5),
        "w_branch_b": nrm(ks[14], (L, B_W, D_MODEL), B_W ** -0.5),
        "w_out": nrm(ks[15], (L, D_MODEL, D_MODEL), D_MODEL ** -0.5),
        "w_router": nrm(ks[16], (L, D_MODEL, N_EXPERTS), D_MODEL ** -0.5),
        "b_router": nrm(ks[17], (L, N_EXPERTS), 0.01),
        "w_gate_up": nrm(ks[18], (L, N_EXPERTS, D_MODEL, 2 * D_EXPERT), D_MODEL ** -0.5),
        "b_gate_up": nrm(ks[19], (L, N_EXPERTS, 2 * D_EXPERT), 0.02),
        "w_down": nrm(ks[20], (L, N_EXPERTS, D_EXPERT, D_MODEL), D_EXPERT ** -0.5),
        "b_down": nrm(ks[21], (L, N_EXPERTS, D_MODEL), 0.02),
    }


def reference(x, c, ctx, c_ctx, w_mod, b_mod, g_pre_mix, g_post_mix, g_pre_ffn, g_post_ffn,
              w_in, a_sink, b_rpb, w_branch_a, w_branch_b, w_out, w_router, b_router,
              w_gate_up, b_gate_up, w_down, b_down):
    cos, sin = axial_rope_tables(x.shape[1])
    for l in range(DEPTH):
        mod_x = (jax.nn.silu(c) @ w_mod[l] + b_mod[l])[:, None, :]
        mod_c = (jax.nn.silu(c_ctx) @ w_mod[l] + b_mod[l])[None, None, :]
        sh1, sc1, g1, sh2, sc2, g2 = jnp.split(mod_x, 6, axis=-1)
        csh1, csc1, cg1, csh2, csc2, cg2 = jnp.split(mod_c, 6, axis=-1)

        hx = modulate(rms_norm(x, g_pre_mix[l]), sh1, sc1)
        hc = modulate(rms_norm(ctx, g_pre_mix[l]), csh1, csc1)
        qa, ka, va, qb, kb, vb, ga, gb = split_projection(hx @ w_in[l])
        cqa, cka, cva, cqb, ckb, cvb, cga, cgb = split_projection(hc @ w_in[l])
        qa = apply_axial_rope(qa, cos, sin)
        ka = apply_axial_rope(ka, cos, sin)
        ya = windowed_sink_attention(qa, ka, va, cka, cva, a_sink[l])
        yb = neighbourhood_attention(qb, kb, vb, ckb, cvb, b_rpb[l])
        mix = merge_branches(ya, yb, ga, gb, w_branch_a[l], w_branch_b[l], w_out[l])
        x = x + g1 * rms_norm(mix, g_post_mix[l])

        hx2 = modulate(rms_norm(x, g_pre_ffn[l]), sh2, sc2)
        ffn = moe_ffn(hx2, w_router[l], b_router[l], w_gate_up[l], b_gate_up[l], w_down[l], b_down[l])
        x = x + g2 * rms_norm(ffn, g_post_ffn[l])

        if l < DEPTH - 1:
            cya = dense_context_attention(cqa, cka, cva, a_sink[l])
            cyb = dense_context_attention(cqb, ckb, cvb, None)
            cmix = merge_branches(cya, cyb, cga, cgb, w_branch_a[l], w_branch_b[l], w_out[l])
            ctx = ctx + cg1 * rms_norm(cmix, g_post_mix[l])
            hc2 = modulate(rms_norm(ctx, g_pre_ffn[l]), csh2, csc2)
            cffn = moe_ffn(hc2, w_router[l], b_router[l], w_gate_up[l], b_gate_up[l], w_down[l], b_down[l])
            ctx = ctx + cg2 * rms_norm(cffn, g_post_ffn[l])
    return x
```

```python
import functools

import jax
import jax.numpy as jnp
import numpy as np
from jax import lax
from jax.experimental import pallas as pl
from jax.experimental.pallas import tpu as pltpu

D_MODEL = 1024
GRID_W = 64
HEAD_DIM = 64
A_HEADS = 8
A_KV_HEADS = 2
A_BLOCK = 128
B_HEADS = 8
B_WIN_ROWS = 8
B_WIN_COLS = 16
N_EXPERTS = 32
TOP_K = 4
SWIGLU_LIMIT = 7.0
SWIGLU_ALPHA = 1.702
ROPE_THETA = 10000.0
NORM_EPS = 1e-6
A_Q = A_HEADS * HEAD_DIM
A_KV = A_KV_HEADS * HEAD_DIM
B_W = B_HEADS * HEAD_DIM
QKV_COLS = A_Q + 2 * A_KV + 3 * B_W

LANES = 128
MASK_VALUE = -1e30
ROUTE_LANES = 128
IDX_LANE, WGT_LANE, RANK_LANE = 0, 4, 8

INPROJ_ROWS = 512
MERGE_ROWS = 256
MOE_ROWS = 256
DISPATCH_ROWS = 128
COMBINE_ROWS = 128
VMEM_LIMIT = 56 * 1024 * 1024

F32 = jnp.float32
BF16 = jnp.bfloat16


def _dot(a, b):
    return jnp.dot(a, b, preferred_element_type=F32)


def _dot_nt(a, b):
    return lax.dot_general(a, b, (((1,), (1,)), ((), ())), preferred_element_type=F32)


def _rms(x, g):
    return x * lax.rsqrt(jnp.mean(x * x, axis=-1, keepdims=True) + NORM_EPS) * g


def _mod_kernel(c_ref, w_ref, b_ref, o_ref):
    c = c_ref[...]
    s = c * jax.nn.sigmoid(c)
    o_ref[...] = jnp.dot(s, w_ref[...], preferred_element_type=F32,
                         precision=lax.Precision.HIGHEST) + b_ref[...]


def _mod_call(cc, w_mod, b_mod):
    rows, d = cc.shape
    n = w_mod.shape[1]
    tn = 1536
    return pl.pallas_call(
        _mod_kernel,
        grid=(n // tn,),
        in_specs=[pl.BlockSpec((rows, d), lambda j: (0, 0)),
                  pl.BlockSpec((d, tn), lambda j: (0, j)),
                  pl.BlockSpec((1, tn), lambda j: (0, j))],
        out_specs=pl.BlockSpec((rows, tn), lambda j: (0, j)),
        out_shape=jax.ShapeDtypeStruct((rows, n), F32),
        compiler_params=pltpu.CompilerParams(dimension_semantics=("arbitrary",),
                                             vmem_limit_bytes=VMEM_LIMIT),
        name="mod",
    )(cc, w_mod, b_mod.reshape(1, n))


def _rope_chunk(c, cos, sin_lo, sin_hi):
    return c * cos + pltpu.roll(c, LANES - 16, 1) * sin_lo + pltpu.roll(c, 16, 1) * sin_hi


def _inproj_latent_kernel(x_ref, sh_ref, sc_ref, g_ref, w_ref, cos_ref, slo_ref, shi_ref,
                          qa_ref, kva_ref, qb_ref, kb_ref, vb_ref):
    scale = HEAD_DIM ** -0.5
    h = (_rms(x_ref[...], g_ref[...]) * (1.0 + sc_ref[0]) + sh_ref[0]).astype(BF16)
    cos, slo, shi = cos_ref[...], slo_ref[...], shi_ref[...]
    a = _dot(h, w_ref[:, 0:A_Q + 2 * A_KV])
    for j in range(A_Q // LANES):
        c = _rope_chunk(a[:, j * LANES:(j + 1) * LANES], cos, slo, shi)
        qa_ref[:, j * LANES:(j + 1) * LANES] = (c * scale).astype(BF16)
    kva_ref[:, 0:A_KV] = _rope_chunk(a[:, A_Q:A_Q + A_KV], cos, slo, shi).astype(BF16)
    kva_ref[:, A_KV:2 * A_KV] = a[:, A_Q + A_KV:A_Q + 2 * A_KV].astype(BF16)
    o = A_Q + 2 * A_KV
    qb_ref[...] = (_dot(h, w_ref[:, o:o + B_W]) * scale).astype(BF16)
    kb_ref[...] = _dot(h, w_ref[:, o + B_W:o + 2 * B_W]).astype(BF16)
    vb_ref[...] = _dot(h, w_ref[:, o + 2 * B_W:o + 3 * B_W]).astype(BF16)


def _inproj_context_kernel(x_ref, sh_ref, sc_ref, g_ref, w_ref, kva_ref, kb_ref, vb_ref):
    h = (_rms(x_ref[...], g_ref[...]) * (1.0 + sc_ref[0]) + sh_ref[0]).astype(BF16)
    kva_ref[...] = _dot(h, w_ref[:, A_Q:A_Q + 2 * A_KV]).astype(BF16)
    o = A_Q + 2 * A_KV
    kb_ref[...] = _dot(h, w_ref[:, o + B_W:o + 2 * B_W]).astype(BF16)
    vb_ref[...] = _dot(h, w_ref[:, o + 2 * B_W:o + 3 * B_W]).astype(BF16)


def _inproj_call(xt, seq, shift, scale, gain, w_qkv, rope_tabs):
    n, d = xt.shape
    tm = min(INPROJ_ROWS, seq)
    per = seq // tm
    shared_mod = shift.shape[0] == 1
    mod_map = (lambda i: (0, 0, 0)) if shared_mod else (lambda i: (i // per, 0, 0))
    in_specs = [pl.BlockSpec((tm, d), lambda i: (i, 0)),
                pl.BlockSpec((1, 1, d), mod_map),
                pl.BlockSpec((1, 1, d), mod_map),
                pl.BlockSpec((1, d), lambda i: (0, 0)),
                pl.BlockSpec(w_qkv.shape, lambda i: (0, 0))]
    args = [xt, shift, scale, gain, w_qkv]
    row = lambda w: pl.BlockSpec((tm, w), lambda i: (i, 0))
    shp = lambda w: jax.ShapeDtypeStruct((n, w), BF16)
    if rope_tabs is not None:
        in_specs += [pl.BlockSpec((tm, LANES), lambda i: (i % per, 0))] * 3
        args += list(rope_tabs)
        kern = _inproj_latent_kernel
        widths = (A_Q, 2 * A_KV, B_W, B_W, B_W)
    else:
        kern = _inproj_context_kernel
        widths = (2 * A_KV, B_W, B_W)
    return pl.pallas_call(
        kern,
        grid=(n // tm,),
        in_specs=in_specs,
        out_specs=[row(w) for w in widths],
        out_shape=[shp(w) for w in widths],
        compiler_params=pltpu.CompilerParams(dimension_semantics=("arbitrary",),
                                             vmem_limit_bytes=VMEM_LIMIT),
        name="inproj_latent" if rope_tabs is not None else "inproj_context",
    )(*args)


def _rope_tables(seq):
    t = np.arange(seq)
    row = (t // GRID_W).astype(np.float32)
    col = (t % GRID_W).astype(np.float32)
    n_freq = HEAD_DIM // 4
    inv_freq = jnp.asarray(ROPE_THETA, F32) ** (-jnp.arange(n_freq, dtype=F32) / n_freq)
    lane = np.arange(LANES)
    d = lane % HEAD_DIM
    use_col = (d // (HEAD_DIM // 2)) == 1
    low_half = (d % (HEAD_DIM // 2)) < n_freq
    pos = jnp.where(use_col[None, :], col[:, None], row[:, None])
    ang = pos * inv_freq[d % n_freq][None, :]
    cos, sin = jnp.cos(ang), jnp.sin(ang)
    return cos, jnp.where(low_half[None, :], -sin, 0.0), jnp.where(low_half[None, :], 0.0, sin)


def _attn_a_kernel(sink_ref, q_ref, kvp_ref, kvc_ref, kvn_ref, kvx_ref, o_ref):
    i = pl.program_id(1)
    nb = pl.num_programs(1)
    r = lax.broadcasted_iota(jnp.int32, (A_BLOCK, A_BLOCK), 0)
    j = lax.broadcasted_iota(jnp.int32, (A_BLOCK, A_BLOCK), 1)
    mask_prev = (j >= r) & (i > 0)
    mask_next = (j <= r) & (i < nb - 1)
    group = A_HEADS // A_KV_HEADS
    for kh in range(A_KV_HEADS):
        ks = slice(kh * HEAD_DIM, (kh + 1) * HEAD_DIM)
        vs = slice(A_KV + kh * HEAD_DIM, A_KV + (kh + 1) * HEAD_DIM)
        kp, kc, kn, kx = kvp_ref[:, ks], kvc_ref[:, ks], kvn_ref[:, ks], kvx_ref[:, ks]
        vp, vc, vn, vx = kvp_ref[:, vs], kvc_ref[:, vs], kvn_ref[:, vs], kvx_ref[:, vs]
        for g in range(group):
            h = kh * group + g
            hs = slice(h * HEAD_DIM, (h + 1) * HEAD_DIM)
            q = q_ref[:, hs]
            sp = jnp.where(mask_prev, _dot_nt(q, kp), MASK_VALUE)
            sc = _dot_nt(q, kc)
            sn = jnp.where(mask_next, _dot_nt(q, kn), MASK_VALUE)
            sx = _dot_nt(q, kx)
            sink = sink_ref[h]
            m = jnp.maximum(jnp.maximum(jnp.max(sp, axis=-1, keepdims=True), jnp.max(sc, axis=-1, keepdims=True)),
                            jnp.maximum(jnp.max(sn, axis=-1, keepdims=True), jnp.max(sx, axis=-1, keepdims=True)))
            m = jnp.maximum(m, sink)
            ep, ec, en, ex = jnp.exp(sp - m), jnp.exp(sc - m), jnp.exp(sn - m), jnp.exp(sx - m)
            denom = (jnp.sum(ep, axis=-1, keepdims=True) + jnp.sum(ec, axis=-1, keepdims=True)
                     + jnp.sum(en, axis=-1, keepdims=True) + jnp.sum(ex, axis=-1, keepdims=True)
                     + jnp.exp(sink - m))
            o = (_dot(ep.astype(BF16), vp) + _dot(ec.astype(BF16), vc)
                 + _dot(en.astype(BF16), vn) + _dot(ex.astype(BF16), vx))
            o_ref[:, hs] = (o / denom).astype(BF16)


def _attn_a_call(sink, qa, kva, kva_ctx, batch, seq, ctx_len):
    nb = seq // A_BLOCK
    kv_spec = lambda f: pl.BlockSpec((A_BLOCK, 2 * A_KV), f)
    return pl.pallas_call(
        _attn_a_kernel,
        grid=(batch, nb),
        in_specs=[pl.BlockSpec(memory_space=pltpu.SMEM),
                  pl.BlockSpec((A_BLOCK, A_Q), lambda b, i: (b * nb + i, 0)),
                  kv_spec(lambda b, i: (b * nb + jnp.maximum(i - 1, 0), 0)),
                  kv_spec(lambda b, i: (b * nb + i, 0)),
                  kv_spec(lambda b, i: (b * nb + jnp.minimum(i + 1, nb - 1), 0)),
                  pl.BlockSpec((ctx_len, 2 * A_KV), lambda b, i: (b, 0))],
        out_specs=pl.BlockSpec((A_BLOCK, A_Q), lambda b, i: (b * nb + i, 0)),
        out_shape=jax.ShapeDtypeStruct(qa.shape, BF16),
        compiler_params=pltpu.CompilerParams(dimension_semantics=("arbitrary", "arbitrary"),
                                             vmem_limit_bytes=VMEM_LIMIT),
        name="attn_a",
    )(sink, qa, kva, kva, kva, kva_ctx)


def _attn_b_kernel(q_ref, k_ref, v_ref, kx_ref, vx_ref, bias_ref, o_ref, *, rows, kr):
    n_loc = kr * GRID_W

    def row_body(r, carry):
        rs = jnp.clip(r - kr // 2, 0, rows - kr)
        pat = r - rs
        q0 = pl.multiple_of(r * GRID_W, GRID_W)
        k0 = pl.multiple_of(rs * GRID_W, GRID_W)
        for h in range(B_HEADS):
            hs = slice(h * HEAD_DIM, (h + 1) * HEAD_DIM)
            q = q_ref[pl.ds(q0, GRID_W), hs]
            s = _dot_nt(q, k_ref[pl.ds(k0, n_loc), hs]) + bias_ref[pat, h]
            sx = _dot_nt(q, kx_ref[:, hs])
            m = jnp.maximum(jnp.max(s, axis=-1, keepdims=True), jnp.max(sx, axis=-1, keepdims=True))
            e, ex = jnp.exp(s - m), jnp.exp(sx - m)
            denom = jnp.sum(e, axis=-1, keepdims=True) + jnp.sum(ex, axis=-1, keepdims=True)
            o = _dot(e.astype(BF16), v_ref[pl.ds(k0, n_loc), hs]) + _dot(ex.astype(BF16), vx_ref[:, hs])
            o_ref[pl.ds(q0, GRID_W), hs] = (o / denom).astype(BF16)
        return carry

    lax.fori_loop(0, rows, row_body, 0)


def _nbr_bias_tables(rpb, rows, kr):
    kc = B_WIN_COLS
    cidx = np.arange(GRID_W)
    col_start = np.clip(cidx - kc // 2, 0, GRID_W - kc)
    col_in = (cidx[None, :] >= col_start[:, None]) & (cidx[None, :] < col_start[:, None] + kc)
    coff = np.clip(cidx[None, :] - cidx[:, None], -(kc - 1), kc - 1) + (B_WIN_COLS - 1)
    roff = np.arange(kr)[None, :] - np.arange(kr)[:, None] + (B_WIN_ROWS - 1)
    bias = rpb.astype(F32)[:, roff[:, None, :, None], coff[None, :, None, :]]
    bias = jnp.where(col_in[None, None, :, None, :], bias, MASK_VALUE)
    return jnp.transpose(bias, (1, 0, 2, 3, 4)).reshape(kr, B_HEADS, GRID_W, kr * GRID_W)


def _attn_b_call(qb, kb, vb, kb_ctx, vb_ctx, bias, batch, seq, ctx_len):
    rows = seq // GRID_W
    kr = min(B_WIN_ROWS, rows)
    tok = lambda: pl.BlockSpec((seq, B_W), lambda b: (b, 0))
    ctx = lambda: pl.BlockSpec((ctx_len, B_W), lambda b: (b, 0))
    return pl.pallas_call(
        functools.partial(_attn_b_kernel, rows=rows, kr=kr),
        grid=(batch,),
        in_specs=[tok(), tok(), tok(), ctx(), ctx(),
                  pl.BlockSpec(bias.shape, lambda b: (0, 0, 0, 0))],
        out_specs=tok(),
        out_shape=jax.ShapeDtypeStruct(qb.shape, BF16),
        compiler_params=pltpu.CompilerParams(dimension_semantics=("arbitrary",),
                                             vmem_limit_bytes=VMEM_LIMIT),
        name="attn_b",
    )(qb, kb, vb, kb_ctx, vb_ctx, bias)


def _merge_kernel(x_ref, ya_ref, yb_ref, sh1_ref, sc1_ref, g1_ref, sh2_ref, sc2_ref,
                  gpre_ref, gpost_ref, gffn_ref, wg_ref, wa_ref, wb_ref, wo_ref, wr_ref, br_ref,
                  x1_ref, h2_ref, route_ref, cnt_ref):
    tm = x_ref.shape[0]

    @pl.when(pl.program_id(0) == 0)
    def _():
        cnt_ref[...] = jnp.zeros_like(cnt_ref)

    x = x_ref[...]
    h = (_rms(x, gpre_ref[...]) * (1.0 + sc1_ref[0]) + sh1_ref[0]).astype(BF16)
    gates = _dot(h, wg_ref[...])
    merged = (jax.nn.sigmoid(gates[:, :D_MODEL]) * _dot(ya_ref[...], wa_ref[...])
              + jax.nn.sigmoid(gates[:, D_MODEL:]) * _dot(yb_ref[...], wb_ref[...]))
    mix = _dot(merged.astype(BF16), wo_ref[...])
    x1 = x + g1_ref[0] * _rms(mix, gpost_ref[...])
    x1_ref[...] = x1
    h2 = _rms(x1, gffn_ref[...]) * (1.0 + sc2_ref[0]) + sh2_ref[0]
    h2_ref[...] = h2

    logits = jnp.dot(h2, wr_ref[...], preferred_element_type=F32, precision=lax.Precision.HIGHEST) + br_ref[...]
    lane = lax.broadcasted_iota(jnp.int32, logits.shape, 1).astype(F32)
    work = logits
    vals, sels, idxs = [], [], []
    for _ in range(TOP_K):
        m = jnp.max(work, axis=-1, keepdims=True)
        idx = jnp.min(jnp.where(work == m, lane, float(N_EXPERTS)), axis=-1, keepdims=True)
        sel = lane == idx
        vals.append(m)
        idxs.append(idx)
        sels.append(sel)
        work = jnp.where(sel, -jnp.inf, work)
    es = [jnp.exp(v - vals[0]) for v in vals]
    esum = es[0]
    for e in es[1:]:
        esum = esum + e

    onehot = sels[0]
    for sel in sels[1:]:
        onehot = onehot | sel
    onehot = jnp.where(onehot, 1.0, 0.0)
    rr = lax.broadcasted_iota(jnp.int32, (tm, tm), 0)
    cc = lax.broadcasted_iota(jnp.int32, (tm, tm), 1)
    tri = jnp.where(cc < rr, 1.0, 0.0).astype(BF16)
    before = _dot(tri, onehot.astype(BF16)) + cnt_ref[...]
    cnt_ref[...] += jnp.sum(onehot, axis=0, keepdims=True)

    out_lane = lax.broadcasted_iota(jnp.int32, (tm, ROUTE_LANES), 1)
    route = jnp.zeros((tm, ROUTE_LANES), F32)
    for k in range(TOP_K):
        rank = jnp.sum(jnp.where(sels[k], before, 0.0), axis=-1, keepdims=True)
        route = jnp.where(out_lane == IDX_LANE + k, idxs[k], route)
        route = jnp.where(out_lane == WGT_LANE + k, es[k] / esum, route)
        route = jnp.where(out_lane == RANK_LANE + k, rank, route)
    route_ref[...] = route


def _merge_call(xt, ya, yb, mods, gpre, gpost, gffn, wg, wa, wb, wo, wr, br, seq):
    n, d = xt.shape
    tm = min(MERGE_ROWS, seq)
    per = seq // tm
    row = lambda w: pl.BlockSpec((tm, w), lambda i: (i, 0))
    mod = lambda: pl.BlockSpec((1, 1, d), lambda i: (i // per, 0, 0))
    full = lambda a: pl.BlockSpec(a.shape, lambda i: (0,) * a.ndim)
    return pl.pallas_call(
        _merge_kernel,
        grid=(n // tm,),
        in_specs=[row(d), row(A_Q), row(B_W)] + [mod() for _ in mods]
                 + [full(a) for a in (gpre, gpost, gffn, wg, wa, wb, wo, wr, br)],
        out_specs=[row(d), row(d), row(ROUTE_LANES), pl.BlockSpec((1, N_EXPERTS), lambda i: (0, 0))],
        out_shape=[jax.ShapeDtypeStruct((n, d), F32), jax.ShapeDtypeStruct((n, d), F32),
                   jax.ShapeDtypeStruct((n, ROUTE_LANES), F32), jax.ShapeDtypeStruct((1, N_EXPERTS), F32)],
        compiler_params=pltpu.CompilerParams(dimension_semantics=("arbitrary",),
                                             vmem_limit_bytes=VMEM_LIMIT),
        name="merge",
    )(xt, ya, yb, *mods, gpre, gpost, gffn, wg, wa, wb, wo, wr, br)


def _dispatch_kernel(dest_ref, h_ref, xpad_in_ref, xpad_ref, sem):
    del xpad_in_ref
    tm = h_ref.shape[0]

    def row_copy(r, d):
        return pltpu.make_async_copy(h_ref.at[pl.ds(r, 1)], xpad_ref.at[pl.ds(d, 1)], sem)

    def issue(r, carry):
        for k in range(TOP_K):
            row_copy(r, dest_ref[0, 0, r * TOP_K + k]).start()
        return carry

    def drain(r, carry):
        for k in range(TOP_K):
            row_copy(r, dest_ref[0, 0, r * TOP_K + k]).wait()
        return carry

    lax.fori_loop(0, tm, issue, 0)
    lax.fori_loop(0, tm, drain, 0)


def _dispatch_call(dest, h2, xpad_zero):
    n, d = h2.shape
    tm = DISPATCH_ROWS
    return pl.pallas_call(
        _dispatch_kernel,
        grid=(n // tm,),
        in_specs=[pl.BlockSpec((1, 1, tm * TOP_K), lambda i: (i, 0, 0), memory_space=pltpu.SMEM),
                  pl.BlockSpec((tm, d), lambda i: (i, 0)),
                  pl.BlockSpec(memory_space=pl.ANY)],
        out_specs=pl.BlockSpec(memory_space=pl.ANY),
        out_shape=jax.ShapeDtypeStruct(xpad_zero.shape, xpad_zero.dtype),
        scratch_shapes=[pltpu.SemaphoreType.DMA(())],
        input_output_aliases={2: 0},
        compiler_params=pltpu.CompilerParams(dimension_semantics=("arbitrary",),
                                             vmem_limit_bytes=VMEM_LIMIT, has_side_effects=True),
        name="dispatch",
    )(dest.reshape(n // tm, 1, tm * TOP_K), h2, xpad_zero)


def _experts_kernel(be_ref, nu_ref, x_ref, wgu_ref, bgu_ref, wd_ref, bd_ref, y_ref, wgu_bf, wd_bf):
    i = pl.program_id(0)
    d = x_ref.shape[1]
    changed = (i == 0) | (be_ref[i] != be_ref[jnp.maximum(i - 1, 0)])

    @pl.when(changed)
    def _():
        chunk = 128

        def cast(c, carry):
            rows = pl.ds(pl.multiple_of(c * chunk, chunk), chunk)
            wgu_bf[rows, :] = wgu_ref[rows, :].astype(BF16)
            wd_bf[rows, :] = wd_ref[rows, :].astype(BF16)
            return carry

        lax.fori_loop(0, d // chunk, cast, 0)

    @pl.when(i < nu_ref[0])
    def _():
        gu = _dot(x_ref[...].astype(BF16), wgu_bf[...]) + bgu_ref[...]
        gate = jnp.minimum(gu[:, :d], SWIGLU_LIMIT)
        up = jnp.clip(gu[:, d:], -SWIGLU_LIMIT, SWIGLU_LIMIT)
        glu = gate * jax.nn.sigmoid(SWIGLU_ALPHA * gate)
        y_ref[...] = _dot(((up + 1.0) * glu).astype(BF16), wd_bf[...]) + bd_ref[...]

    @pl.when(i >= nu_ref[0])
    def _():
        y_ref[...] = jnp.zeros_like(y_ref)


def _experts_call(block_expert, n_used, xpad, w_gate_up, b_gate_up, w_down, b_down):
    p, d = xpad.shape
    n_blocks = p // MOE_ROWS
    e, _, d2 = w_gate_up.shape
    grid_spec = pltpu.PrefetchScalarGridSpec(
        num_scalar_prefetch=2,
        grid=(n_blocks,),
        in_specs=[pl.BlockSpec((MOE_ROWS, d), lambda i, be, nu: (i, 0)),
                  pl.BlockSpec((None, d, d2), lambda i, be, nu: (be[i], 0, 0)),
                  pl.BlockSpec((None, 1, d2), lambda i, be, nu: (be[i], 0, 0)),
                  pl.BlockSpec((None, d, d), lambda i, be, nu: (be[i], 0, 0)),
                  pl.BlockSpec((None, 1, d), lambda i, be, nu: (be[i], 0, 0))],
        out_specs=pl.BlockSpec((MOE_ROWS, d), lambda i, be, nu: (i, 0)),
        scratch_shapes=[pltpu.VMEM((d, d2), BF16), pltpu.VMEM((d, d), BF16)],
    )
    return pl.pallas_call(
        _experts_kernel,
        grid_spec=grid_spec,
        out_shape=jax.ShapeDtypeStruct((p, d), F32),
        compiler_params=pltpu.CompilerParams(dimension_semantics=("arbitrary",),
                                             vmem_limit_bytes=VMEM_LIMIT),
        name="experts",
    )(block_expert, n_used, xpad, w_gate_up, b_gate_up.reshape(e, 1, d2), w_down, b_down.reshape(e, 1, d))


def _combine_kernel(dest_ref, route_ref, x1_ref, g2_ref, gpost_ref, ypad_ref, o_ref, buf, sem):
    tm = x1_ref.shape[0]

    def row_copy(r, k, d):
        return pltpu.make_async_copy(ypad_ref.at[pl.ds(d, 1)], buf.at[k, pl.ds(r, 1)], sem)

    def issue(r, carry):
        for k in range(TOP_K):
            row_copy(r, k, dest_ref[0, 0, r * TOP_K + k]).start()
        return carry

    def drain(r, carry):
        for k in range(TOP_K):
            row_copy(r, k, dest_ref[0, 0, r * TOP_K + k]).wait()
        return carry

    lax.fori_loop(0, tm, issue, 0)
    lax.fori_loop(0, tm, drain, 0)
    route = route_ref[...]
    y = route[:, WGT_LANE:WGT_LANE + 1] * buf[0]
    for k in range(1, TOP_K):
        y = y + route[:, WGT_LANE + k:WGT_LANE + k + 1] * buf[k]
    o_ref[...] = x1_ref[...] + g2_ref[0] * _rms(y, gpost_ref[...])


def _combine_call(dest, route, x1, g2, gpost, ypad, seq):
    n, d = x1.shape
    tm = COMBINE_ROWS
    per = seq // tm
    return pl.pallas_call(
        _combine_kernel,
        grid=(n // tm,),
        in_specs=[pl.BlockSpec((1, 1, tm * TOP_K), lambda i: (i, 0, 0), memory_space=pltpu.SMEM),
                  pl.BlockSpec((tm, ROUTE_LANES), lambda i: (i, 0)),
                  pl.BlockSpec((tm, d), lambda i: (i, 0)),
                  pl.BlockSpec((1, 1, d), lambda i: (i // per, 0, 0)),
                  pl.BlockSpec((1, d), lambda i: (0, 0)),
                  pl.BlockSpec(memory_space=pl.ANY)],
        out_specs=pl.BlockSpec((tm, d), lambda i: (i, 0)),
        out_shape=jax.ShapeDtypeStruct((n, d), F32),
        scratch_shapes=[pltpu.VMEM((TOP_K, tm, d), F32), pltpu.SemaphoreType.DMA(())],
        compiler_params=pltpu.CompilerParams(dimension_semantics=("arbitrary",),
                                             vmem_limit_bytes=VMEM_LIMIT),
        name="combine",
    )(dest.reshape(n // tm, 1, tm * TOP_K), route, x1, g2, gpost, ypad)


def _layer(x, ctx, mod, g_pre_mix, g_post_mix, g_pre_ffn, g_post_ffn, w_in, a_sink, b_rpb,
           w_branch_a, w_branch_b, w_out, w_router, b_router, w_gate_up, b_gate_up, w_down, b_down):
    batch, seq, d = x.shape
    ctx_len = ctx.shape[1]
    n = batch * seq
    xt = x.reshape(n, d)
    row = lambda v: v.reshape(1, -1)

    mod_x = mod[:batch].reshape(batch, 1, 6, d)
    sh1, sc1, g1, sh2, sc2, g2 = (mod_x[:, :, k] for k in range(6))
    mod_c = mod[batch:batch + 1].reshape(1, 1, 6, d)
    csh1, csc1 = mod_c[:, :, 0], mod_c[:, :, 1]

    w_qkv = w_in[:, :QKV_COLS].astype(BF16)
    w_gate = w_in[:, QKV_COLS:].astype(BF16)
    qa, kva, qb, kb, vb = _inproj_call(xt, seq, sh1, sc1, row(g_pre_mix), w_qkv, _rope_tables(seq))
    kva_c, kb_c, vb_c = _inproj_call(ctx.reshape(batch * ctx_len, d), ctx_len, csh1, csc1,
                                     row(g_pre_mix), w_qkv, None)

    ya = _attn_a_call(a_sink.astype(F32), qa, kva, kva_c, batch, seq, ctx_len)
    rows = seq // GRID_W
    bias = _nbr_bias_tables(b_rpb, rows, min(B_WIN_ROWS, rows))
    yb = _attn_b_call(qb, kb, vb, kb_c, vb_c, bias, batch, seq, ctx_len)

    x1, h2, route, counts = _merge_call(
        xt, ya, yb, (sh1, sc1, g1, sh2, sc2), row(g_pre_mix), row(g_post_mix), row(g_pre_ffn),
        w_gate, w_branch_a.astype(BF16), w_branch_b.astype(BF16), w_out.astype(BF16),
        w_router, row(b_router), seq)

    counts = counts.reshape(N_EXPERTS).astype(jnp.int32)
    padded = ((counts + MOE_ROWS - 1) // MOE_ROWS) * MOE_ROWS
    pend = jnp.cumsum(padded)
    pstart = pend - padded
    top_idx = route[:, IDX_LANE:IDX_LANE + TOP_K].astype(jnp.int32)
    rank = route[:, RANK_LANE:RANK_LANE + TOP_K].astype(jnp.int32)
    dest = pstart[top_idx] + rank
    n_blocks = (n * TOP_K) // MOE_ROWS + N_EXPERTS
    n_used = pend[-1] // MOE_ROWS
    blk = jnp.arange(n_blocks, dtype=jnp.int32)
    block_expert = jnp.clip(jnp.searchsorted(pend, blk * MOE_ROWS, side="right"), 0, N_EXPERTS - 1)
    block_expert = jnp.where(blk < n_used, block_expert, block_expert[jnp.maximum(n_used - 1, 0)])

    xpad = _dispatch_call(dest, h2, jnp.zeros((n_blocks * MOE_ROWS, d), F32))
    ypad = _experts_call(block_expert.astype(jnp.int32), n_used.reshape(1).astype(jnp.int32), xpad,
                         w_gate_up, b_gate_up, w_down, b_down)
    out = _combine_call(dest, route, x1, g2, row(g_post_ffn), ypad, seq)
    return out.reshape(batch, seq, d)


def kernel(x, c, ctx, c_ctx, w_mod, b_mod, g_pre_mix, g_post_mix, g_pre_ffn, g_post_ffn, w_in, a_sink, b_rpb,
           w_branch_a, w_branch_b, w_out, w_router, b_router, w_gate_up, b_gate_up, w_down, b_down):
    depth = w_mod.shape[0]
    batch = x.shape[0]
    assert depth == 1, "context-stream update between layers is not implemented"
    cc = jnp.zeros((16, D_MODEL), F32).at[:batch].set(c).at[batch].set(c_ctx)
    for l in range(depth):
        mod = _mod_call(cc, w_mod[l], b_mod[l])
        x = _layer(x, ctx, mod, g_pre_mix[l], g_post_mix[l], g_pre_ffn[l], g_post_ffn[l], w_in[l], a_sink[l],
                   b_rpb[l], w_branch_a[l], w_branch_b[l], w_out[l], w_router[l], b_router[l],
                   w_gate_up[l], b_gate_up[l], w_down[l], b_down[l])
    return x
```

```python
import functools

import jax
import jax.numpy as jnp
import numpy as np
from jax import lax
from jax.experimental import pallas as pl
from jax.experimental.pallas import tpu as pltpu

D_MODEL = 1024
GRID_W = 64
HEAD_DIM = 64
A_HEADS = 8
A_KV_HEADS = 2
A_BLOCK = 128
B_HEADS = 8
B_WIN_ROWS = 8
B_WIN_COLS = 16
N_EXPERTS = 32
TOP_K = 4
SWIGLU_LIMIT = 7.0
SWIGLU_ALPHA = 1.702
ROPE_THETA = 10000.0
NORM_EPS = 1e-6
A_Q = A_HEADS * HEAD_DIM
A_KV = A_KV_HEADS * HEAD_DIM
B_W = B_HEADS * HEAD_DIM
QKV_COLS = A_Q + 2 * A_KV + 3 * B_W

LANES = 128
MASK_VALUE = -1e30
ROUTE_LANES = 128
IDX_LANE, WGT_LANE, RANK_LANE = 0, 4, 8

INPROJ_ROWS = 512
MERGE_ROWS = 256
MOE_ROWS = 256
DISPATCH_ROWS = 128
COMBINE_ROWS = 128
VMEM_LIMIT = 56 * 1024 * 1024

F32 = jnp.float32
BF16 = jnp.bfloat16


def _dot(a, b):
    return jnp.dot(a, b, preferred_element_type=F32)


def _dot_nt(a, b):
    return lax.dot_general(a, b, (((1,), (1,)), ((), ())), preferred_element_type=F32)


def _rms(x, g):
    return x * lax.rsqrt(jnp.mean(x * x, axis=-1, keepdims=True) + NORM_EPS) * g


def _mod_kernel(c_ref, w_ref, b_ref, o_ref):
    c = c_ref[...]
    s = c * jax.nn.sigmoid(c)
    o_ref[...] = jnp.dot(s, w_ref[...], preferred_element_type=F32,
                         precision=lax.Precision.HIGHEST) + b_ref[...]


def _mod_call(cc, w_mod, b_mod):
    rows, d = cc.shape
    n = w_mod.shape[1]
    tn = 1536
    return pl.pallas_call(
        _mod_kernel,
        grid=(n // tn,),
        in_specs=[pl.BlockSpec((rows, d), lambda j: (0, 0)),
                  pl.BlockSpec((d, tn), lambda j: (0, j)),
                  pl.BlockSpec((1, tn), lambda j: (0, j))],
        out_specs=pl.BlockSpec((rows, tn), lambda j: (0, j)),
        out_shape=jax.ShapeDtypeStruct((rows, n), F32),
        compiler_params=pltpu.CompilerParams(dimension_semantics=("arbitrary",),
                                             vmem_limit_bytes=VMEM_LIMIT),
        name="mod",
    )(cc, w_mod, b_mod.reshape(1, n))


def _rope_chunk(c, cos, sin_lo, sin_hi):
    return c * cos + pltpu.roll(c, LANES - 16, 1) * sin_lo + pltpu.roll(c, 16, 1) * sin_hi


def _inproj_latent_kernel(x_ref, sh_ref, sc_ref, g_ref, w_ref, cos_ref, slo_ref, shi_ref,
                          qa_ref, kva_ref, qb_ref, kb_ref, vb_ref):
    scale = HEAD_DIM ** -0.5
    h = (_rms(x_ref[...], g_ref[...]) * (1.0 + sc_ref[0]) + sh_ref[0]).astype(BF16)
    cos, slo, shi = cos_ref[...], slo_ref[...], shi_ref[...]
    a = _dot(h, w_ref[:, 0:A_Q + 2 * A_KV])
    for j in range(A_Q // LANES):
        c = _rope_chunk(a[:, j * LANES:(j + 1) * LANES], cos, slo, shi)
        qa_ref[:, j * LANES:(j + 1) * LANES] = (c * scale).astype(BF16)
    kva_ref[:, 0:A_KV] = _rope_chunk(a[:, A_Q:A_Q + A_KV], cos, slo, shi).astype(BF16)
    kva_ref[:, A_KV:2 * A_KV] = a[:, A_Q + A_KV:A_Q + 2 * A_KV].astype(BF16)
    o = A_Q + 2 * A_KV
    qb_ref[...] = (_dot(h, w_ref[:, o:o + B_W]) * scale).astype(BF16)
    kb_ref[...] = _dot(h, w_ref[:, o + B_W:o + 2 * B_W]).astype(BF16)
    vb_ref[...] = _dot(h, w_ref[:, o + 2 * B_W:o + 3 * B_W]).astype(BF16)


def _inproj_context_kernel(x_ref, sh_ref, sc_ref, g_ref, w_ref, kva_ref, kb_ref, vb_ref):
    h = (_rms(x_ref[...], g_ref[...]) * (1.0 + sc_ref[0]) + sh_ref[0]).astype(BF16)
    kva_ref[...] = _dot(h, w_ref[:, A_Q:A_Q + 2 * A_KV]).astype(BF16)
    o = A_Q + 2 * A_KV
    kb_ref[...] = _dot(h, w_ref[:, o + B_W:o + 2 * B_W]).astype(BF16)
    vb_ref[...] = _dot(h, w_ref[:, o + 2 * B_W:o + 3 * B_W]).astype(BF16)


def _inproj_call(xt, seq, shift, scale, gain, w_qkv, rope_tabs):
    n, d = xt.shape
    tm = min(INPROJ_ROWS, seq)
    per = seq // tm
    shared_mod = shift.shape[0] == 1
    mod_map = (lambda i: (0, 0, 0)) if shared_mod else (lambda i: (i // per, 0, 0))
    in_specs = [pl.BlockSpec((tm, d), lambda i: (i, 0)),
                pl.BlockSpec((1, 1, d), mod_map),
                pl.BlockSpec((1, 1, d), mod_map),
                pl.BlockSpec((1, d), lambda i: (0, 0)),
                pl.BlockSpec(w_qkv.shape, lambda i: (0, 0))]
    args = [xt, shift, scale, gain, w_qkv]
    row = lambda w: pl.BlockSpec((tm, w), lambda i: (i, 0))
    shp = lambda w: jax.ShapeDtypeStruct((n, w), BF16)
    if rope_tabs is not None:
        in_specs += [pl.BlockSpec((tm, LANES), lambda i: (i % per, 0))] * 3
        args += list(rope_tabs)
        kern = _inproj_latent_kernel
        widths = (A_Q, 2 * A_KV, B_W, B_W, B_W)
    else:
        kern = _inproj_context_kernel
        widths = (2 * A_KV, B_W, B_W)
    return pl.pallas_call(
        kern,
        grid=(n // tm,),
        in_specs=in_specs,
        out_specs=[row(w) for w in widths],
        out_shape=[shp(w) for w in widths],
        compiler_params=pltpu.CompilerParams(dimension_semantics=("arbitrary",),
                                             vmem_limit_bytes=VMEM_LIMIT),
        name="inproj_latent" if rope_tabs is not None else "inproj_context",
    )(*args)


def _rope_tables(seq):
    t = np.arange(seq)
    row = (t // GRID_W).astype(np.float32)
    col = (t % GRID_W).astype(np.float32)
    n_freq = HEAD_DIM // 4
    inv_freq = jnp.asarray(ROPE_THETA, F32) ** (-jnp.arange(n_freq, dtype=F32) / n_freq)
    lane = np.arange(LANES)
    d = lane % HEAD_DIM
    use_col = (d // (HEAD_DIM // 2)) == 1
    low_half = (d % (HEAD_DIM // 2)) < n_freq
    pos = jnp.where(use_col[None, :], col[:, None], row[:, None])
    ang = pos * inv_freq[d % n_freq][None, :]
    cos, sin = jnp.cos(ang), jnp.sin(ang)
    return cos, jnp.where(low_half[None, :], -sin, 0.0), jnp.where(low_half[None, :], 0.0, sin)


def _split_head_pair(q):
    low = lax.broadcasted_iota(jnp.int32, q.shape, 1) < HEAD_DIM
    zero = jnp.zeros_like(q)
    return jnp.concatenate([jnp.where(low, q, zero), jnp.where(low, zero, q)], axis=0)


def _join_head_pair(pv):
    m = pv.shape[0] // 2
    low = lax.broadcasted_iota(jnp.int32, (m, LANES), 1) < HEAD_DIM
    return jnp.where(low, pv[:m], pv[m:])


def _attn_a_kernel(sink_ref, q_ref, kvp_ref, kvc_ref, kvn_ref, kvx_ref, o_ref):
    i = pl.program_id(1)
    nb = pl.num_programs(1)
    group = A_HEADS // A_KV_HEADS
    r2 = lax.broadcasted_iota(jnp.int32, (2 * A_BLOCK, A_BLOCK), 0)
    r = jnp.where(r2 >= A_BLOCK, r2 - A_BLOCK, r2)
    j = lax.broadcasted_iota(jnp.int32, (2 * A_BLOCK, A_BLOCK), 1)
    mask_prev = (j >= r) & (i > 0)
    mask_next = (j <= r) & (i < nb - 1)
    upper = lax.broadcasted_iota(jnp.int32, (2 * A_BLOCK, 1), 0) >= A_BLOCK
    kp, kc, kn, kx = kvp_ref[:, :A_KV], kvc_ref[:, :A_KV], kvn_ref[:, :A_KV], kvx_ref[:, :A_KV]
    vp, vc, vn, vx = kvp_ref[:, A_KV:], kvc_ref[:, A_KV:], kvn_ref[:, A_KV:], kvx_ref[:, A_KV:]
    b = A_BLOCK
    for g in range(group):
        cs = slice(g * LANES, (g + 1) * LANES)
        q2 = _split_head_pair(q_ref[:, cs])
        s = jnp.concatenate([jnp.where(mask_prev, _dot_nt(q2, kp), MASK_VALUE), _dot_nt(q2, kc),
                             jnp.where(mask_next, _dot_nt(q2, kn), MASK_VALUE), _dot_nt(q2, kx)], axis=-1)
        sink = jnp.where(upper, sink_ref[group + g], sink_ref[g])
        m = jnp.maximum(jnp.max(s, axis=-1, keepdims=True), sink)
        e = jnp.exp(s - m)
        denom = jnp.sum(e, axis=-1, keepdims=True) + jnp.exp(sink - m)
        e = e.astype(BF16)
        pv = _dot(e[:, 0:b], vp) + _dot(e[:, b:2 * b], vc) + _dot(e[:, 2 * b:3 * b], vn) + _dot(e[:, 3 * b:], vx)
        o_ref[:, cs] = _join_head_pair(pv / denom).astype(BF16)


def _attn_a_call(sink, qa, kva, kva_ctx, batch, seq, ctx_len):
    nb = seq // A_BLOCK
    kv_spec = lambda f: pl.BlockSpec((A_BLOCK, 2 * A_KV), f)
    return pl.pallas_call(
        _attn_a_kernel,
        grid=(batch, nb),
        in_specs=[pl.BlockSpec(memory_space=pltpu.SMEM),
                  pl.BlockSpec((A_BLOCK, A_Q), lambda b, i: (b * nb + i, 0)),
                  kv_spec(lambda b, i: (b * nb + jnp.maximum(i - 1, 0), 0)),
                  kv_spec(lambda b, i: (b * nb + i, 0)),
                  kv_spec(lambda b, i: (b * nb + jnp.minimum(i + 1, nb - 1), 0)),
                  pl.BlockSpec((ctx_len, 2 * A_KV), lambda b, i: (b, 0))],
        out_specs=pl.BlockSpec((A_BLOCK, A_Q), lambda b, i: (b * nb + i, 0)),
        out_shape=jax.ShapeDtypeStruct(qa.shape, BF16),
        compiler_params=pltpu.CompilerParams(dimension_semantics=("arbitrary", "arbitrary"),
                                             vmem_limit_bytes=VMEM_LIMIT),
        name="attn_a",
    )(sink, qa, kva, kva, kva, kva_ctx)


def _attn_b_kernel(q_ref, k_ref, v_ref, kx_ref, vx_ref, bias_ref, o_ref, *, rows, kr):
    n_loc = kr * GRID_W

    def row_body(r, carry):
        rs = jnp.clip(r - kr // 2, 0, rows - kr)
        pat = r - rs
        q0 = pl.multiple_of(r * GRID_W, GRID_W)
        k0 = pl.multiple_of(rs * GRID_W, GRID_W)
        for j in range(B_HEADS // 2):
            cs = slice(j * LANES, (j + 1) * LANES)
            q2 = _split_head_pair(q_ref[pl.ds(q0, GRID_W), cs])
            s = jnp.concatenate([_dot_nt(q2, k_ref[pl.ds(k0, n_loc), cs]) + bias_ref[pat, j],
                                 _dot_nt(q2, kx_ref[:, cs])], axis=-1)
            m = jnp.max(s, axis=-1, keepdims=True)
            e = jnp.exp(s - m)
            denom = jnp.sum(e, axis=-1, keepdims=True)
            e = e.astype(BF16)
            pv = _dot(e[:, :n_loc], v_ref[pl.ds(k0, n_loc), cs]) + _dot(e[:, n_loc:], vx_ref[:, cs])
            o_ref[pl.ds(q0, GRID_W), cs] = _join_head_pair(pv / denom).astype(BF16)
        return carry

    lax.fori_loop(0, rows, row_body, 0)


def _nbr_bias_tables(rpb, rows, kr):
    kc = B_WIN_COLS
    cidx = np.arange(GRID_W)
    col_start = np.clip(cidx - kc // 2, 0, GRID_W - kc)
    col_in = (cidx[None, :] >= col_start[:, None]) & (cidx[None, :] < col_start[:, None] + kc)
    rpb = rpb.astype(F32)
    edge = GRID_W - kc
    ext = jnp.concatenate([jnp.repeat(rpb[..., :1], edge, axis=-1), rpb,
                           jnp.repeat(rpb[..., -1:], edge, axis=-1)], axis=-1)
    toep = jnp.stack([ext[..., GRID_W - 1 - q:2 * GRID_W - 1 - q] for q in range(GRID_W)], axis=2)
    toep = jnp.where(col_in[None, None], toep, MASK_VALUE)
    per_pat = [toep[:, B_WIN_ROWS - 1 - p:B_WIN_ROWS - 1 - p + kr] for p in range(kr)]
    bias = jnp.stack(per_pat, axis=0)
    return jnp.transpose(bias, (0, 1, 3, 2, 4)).reshape(kr, B_HEADS // 2, 2 * GRID_W, kr * GRID_W)


def _attn_b_call(qb, kb, vb, kb_ctx, vb_ctx, bias, batch, seq, ctx_len):
    rows = seq // GRID_W
    kr = min(B_WIN_ROWS, rows)
    tok = lambda: pl.BlockSpec((seq, B_W), lambda b: (b, 0))
    ctx = lambda: pl.BlockSpec((ctx_len, B_W), lambda b: (b, 0))
    return pl.pallas_call(
        functools.partial(_attn_b_kernel, rows=rows, kr=kr),
        grid=(batch,),
        in_specs=[tok(), tok(), tok(), ctx(), ctx(),
                  pl.BlockSpec(bias.shape, lambda b: (0, 0, 0, 0))],
        out_specs=tok(),
        out_shape=jax.ShapeDtypeStruct(qb.shape, BF16),
        compiler_params=pltpu.CompilerParams(dimension_semantics=("arbitrary",),
                                             vmem_limit_bytes=VMEM_LIMIT),
        name="attn_b",
    )(qb, kb, vb, kb_ctx, vb_ctx, bias)


def _merge_kernel(x_ref, ya_ref, yb_ref, sh1_ref, sc1_ref, g1_ref, sh2_ref, sc2_ref,
                  gpre_ref, gpost_ref, gffn_ref, wg_ref, wa_ref, wb_ref, wo_ref, wr_ref, br_ref,
                  x1_ref, h2_ref, route_ref, cnt_ref):
    tm = x_ref.shape[0]

    @pl.when(pl.program_id(0) == 0)
    def _():
        cnt_ref[...] = jnp.zeros_like(cnt_ref)

    x = x_ref[...]
    h = (_rms(x, gpre_ref[...]) * (1.0 + sc1_ref[0]) + sh1_ref[0]).astype(BF16)
    gates = _dot(h, wg_ref[...])
    merged = (jax.nn.sigmoid(gates[:, :D_MODEL]) * _dot(ya_ref[...], wa_ref[...])
              + jax.nn.sigmoid(gates[:, D_MODEL:]) * _dot(yb_ref[...], wb_ref[...]))
    mix = _dot(merged.astype(BF16), wo_ref[...])
    x1 = x + g1_ref[0] * _rms(mix, gpost_ref[...])
    x1_ref[...] = x1
    h2 = _rms(x1, gffn_ref[...]) * (1.0 + sc2_ref[0]) + sh2_ref[0]
    h2_ref[...] = h2

    logits = jnp.dot(h2, wr_ref[...], preferred_element_type=F32, precision=lax.Precision.HIGHEST) + br_ref[...]
    lane = lax.broadcasted_iota(jnp.int32, logits.shape, 1).astype(F32)
    work = logits
    vals, sels, idxs = [], [], []
    for _ in range(TOP_K):
        m = jnp.max(work, axis=-1, keepdims=True)
        idx = jnp.min(jnp.where(work == m, lane, float(N_EXPERTS)), axis=-1, keepdims=True)
        sel = lane == idx
        vals.append(m)
        idxs.append(idx)
        sels.append(sel)
        work = jnp.where(sel, -jnp.inf, work)
    es = [jnp.exp(v - vals[0]) for v in vals]
    esum = es[0]
    for e in es[1:]:
        esum = esum + e

    onehot = sels[0]
    for sel in sels[1:]:
        onehot = onehot | sel
    onehot = jnp.where(onehot, 1.0, 0.0)
    rr = lax.broadcasted_iota(jnp.int32, (tm, tm), 0)
    cc = lax.broadcasted_iota(jnp.int32, (tm, tm), 1)
    tri = jnp.where(cc < rr, 1.0, 0.0).astype(BF16)
    before = _dot(tri, onehot.astype(BF16)) + cnt_ref[...]
    cnt_ref[...] += jnp.sum(onehot, axis=0, keepdims=True)

    out_lane = lax.broadcasted_iota(jnp.int32, (tm, ROUTE_LANES), 1)
    route = jnp.zeros((tm, ROUTE_LANES), F32)
    for k in range(TOP_K):
        rank = jnp.sum(jnp.where(sels[k], before, 0.0), axis=-1, keepdims=True)
        route = jnp.where(out_lane == IDX_LANE + k, idxs[k], route)
        route = jnp.where(out_lane == WGT_LANE + k, es[k] / esum, route)
        route = jnp.where(out_lane == RANK_LANE + k, rank, route)
    route_ref[...] = route


def _merge_call(xt, ya, yb, mods, gpre, gpost, gffn, wg, wa, wb, wo, wr, br, seq):
    n, d = xt.shape
    tm = min(MERGE_ROWS, seq)
    per = seq // tm
    row = lambda w: pl.BlockSpec((tm, w), lambda i: (i, 0))
    mod = lambda: pl.BlockSpec((1, 1, d), lambda i: (i // per, 0, 0))
    full = lambda a: pl.BlockSpec(a.shape, lambda i: (0,) * a.ndim)
    return pl.pallas_call(
        _merge_kernel,
        grid=(n // tm,),
        in_specs=[row(d), row(A_Q), row(B_W)] + [mod() for _ in mods]
                 + [full(a) for a in (gpre, gpost, gffn, wg, wa, wb, wo, wr, br)],
        out_specs=[row(d), row(d), row(ROUTE_LANES), pl.BlockSpec((1, N_EXPERTS), lambda i: (0, 0))],
        out_shape=[jax.ShapeDtypeStruct((n, d), F32), jax.ShapeDtypeStruct((n, d), F32),
                   jax.ShapeDtypeStruct((n, ROUTE_LANES), F32), jax.ShapeDtypeStruct((1, N_EXPERTS), F32)],
        compiler_params=pltpu.CompilerParams(dimension_semantics=("arbitrary",),
                                             vmem_limit_bytes=VMEM_LIMIT),
        name="merge",
    )(xt, ya, yb, *mods, gpre, gpost, gffn, wg, wa, wb, wo, wr, br)


def _dispatch_kernel(dest_ref, h_ref, xpad_in_ref, xpad_ref, sem):
    del xpad_in_ref
    tm = h_ref.shape[0]

    def row_copy(r, d):
        return pltpu.make_async_copy(h_ref.at[pl.ds(r, 1)], xpad_ref.at[pl.ds(d, 1)], sem)

    def issue(r, carry):
        for k in range(TOP_K):
            row_copy(r, dest_ref[0, 0, r * TOP_K + k]).start()
        return carry

    def drain(r, carry):
        for k in range(TOP_K):
            row_copy(r, dest_ref[0, 0, r * TOP_K + k]).wait()
        return carry

    lax.fori_loop(0, tm, issue, 0)
    lax.fori_loop(0, tm, drain, 0)


def _dispatch_call(dest, h2, xpad_zero):
    n, d = h2.shape
    tm = DISPATCH_ROWS
    return pl.pallas_call(
        _dispatch_kernel,
        grid=(n // tm,),
        in_specs=[pl.BlockSpec((1, 1, tm * TOP_K), lambda i: (i, 0, 0), memory_space=pltpu.SMEM),
                  pl.BlockSpec((tm, d), lambda i: (i, 0)),
                  pl.BlockSpec(memory_space=pl.ANY)],
        out_specs=pl.BlockSpec(memory_space=pl.ANY),
        out_shape=jax.ShapeDtypeStruct(xpad_zero.shape, xpad_zero.dtype),
        scratch_shapes=[pltpu.SemaphoreType.DMA(())],
        input_output_aliases={2: 0},
        compiler_params=pltpu.CompilerParams(dimension_semantics=("arbitrary",),
                                             vmem_limit_bytes=VMEM_LIMIT, has_side_effects=True),
        name="dispatch",
    )(dest.reshape(n // tm, 1, tm * TOP_K), h2, xpad_zero)


def _experts_kernel(be_ref, nu_ref, x_ref, wgu_ref, bgu_ref, wd_ref, bd_ref, y_ref, wgu_bf, wd_bf):
    i = pl.program_id(0)
    d = x_ref.shape[1]
    changed = (i == 0) | (be_ref[i] != be_ref[jnp.maximum(i - 1, 0)])

    @pl.when(changed)
    def _():
        chunk = 128

        def cast(c, carry):
            rows = pl.ds(pl.multiple_of(c * chunk, chunk), chunk)
            wgu_bf[rows, :] = wgu_ref[rows, :].astype(BF16)
            wd_bf[rows, :] = wd_ref[rows, :].astype(BF16)
            return carry

        lax.fori_loop(0, d // chunk, cast, 0)

    @pl.when(i < nu_ref[0])
    def _():
        gu = _dot(x_ref[...].astype(BF16), wgu_bf[...]) + bgu_ref[...]
        gate = jnp.minimum(gu[:, :d], SWIGLU_LIMIT)
        up = jnp.clip(gu[:, d:], -SWIGLU_LIMIT, SWIGLU_LIMIT)
        glu = gate * jax.nn.sigmoid(SWIGLU_ALPHA * gate)
        y_ref[...] = _dot(((up + 1.0) * glu).astype(BF16), wd_bf[...]) + bd_ref[...]

    @pl.when(i >= nu_ref[0])
    def _():
        y_ref[...] = jnp.zeros_like(y_ref)


def _experts_call(block_expert, n_used, xpad, w_gate_up, b_gate_up, w_down, b_down):
    p, d = xpad.shape
    n_blocks = p // MOE_ROWS
    e, _, d2 = w_gate_up.shape
    grid_spec = pltpu.PrefetchScalarGridSpec(
        num_scalar_prefetch=2,
        grid=(n_blocks,),
        in_specs=[pl.BlockSpec((MOE_ROWS, d), lambda i, be, nu: (i, 0)),
                  pl.BlockSpec((None, d, d2), lambda i, be, nu: (be[i], 0, 0)),
                  pl.BlockSpec((None, 1, d2), lambda i, be, nu: (be[i], 0, 0)),
                  pl.BlockSpec((None, d, d), lambda i, be, nu: (be[i], 0, 0)),
                  pl.BlockSpec((None, 1, d), lambda i, be, nu: (be[i], 0, 0))],
        out_specs=pl.BlockSpec((MOE_ROWS, d), lambda i, be, nu: (i, 0)),
        scratch_shapes=[pltpu.VMEM((d, d2), BF16), pltpu.VMEM((d, d), BF16)],
    )
    return pl.pallas_call(
        _experts_kernel,
        grid_spec=grid_spec,
        out_shape=jax.ShapeDtypeStruct((p, d), F32),
        compiler_params=pltpu.CompilerParams(dimension_semantics=("arbitrary",),
                                             vmem_limit_bytes=VMEM_LIMIT),
        name="experts",
    )(block_expert, n_used, xpad, w_gate_up, b_gate_up.reshape(e, 1, d2), w_down, b_down.reshape(e, 1, d))


def _combine_kernel(dest_ref, route_ref, x1_ref, g2_ref, gpost_ref, ypad_ref, o_ref, buf, sem):
    tm = x1_ref.shape[0]

    def row_copy(r, k, d):
        return pltpu.make_async_copy(ypad_ref.at[pl.ds(d, 1)], buf.at[k, pl.ds(r, 1)], sem)

    def issue(r, carry):
        for k in range(TOP_K):
            row_copy(r, k, dest_ref[0, 0, r * TOP_K + k]).start()
        return carry

    def drain(r, carry):
        for k in range(TOP_K):
            row_copy(r, k, dest_ref[0, 0, r * TOP_K + k]).wait()
        return carry

    lax.fori_loop(0, tm, issue, 0)
    lax.fori_loop(0, tm, drain, 0)
    route = route_ref[...]
    y = route[:, WGT_LANE:WGT_LANE + 1] * buf[0]
    for k in range(1, TOP_K):
        y = y + route[:, WGT_LANE + k:WGT_LANE + k + 1] * buf[k]
    o_ref[...] = x1_ref[...] + g2_ref[0] * _rms(y, gpost_ref[...])


def _combine_call(dest, route, x1, g2, gpost, ypad, seq):
    n, d = x1.shape
    tm = COMBINE_ROWS
    per = seq // tm
    return pl.pallas_call(
        _combine_kernel,
        grid=(n // tm,),
        in_specs=[pl.BlockSpec((1, 1, tm * TOP_K), lambda i: (i, 0, 0), memory_space=pltpu.SMEM),
                  pl.BlockSpec((tm, ROUTE_LANES), lambda i: (i, 0)),
                  pl.BlockSpec((tm, d), lambda i: (i, 0)),
                  pl.BlockSpec((1, 1, d), lambda i: (i // per, 0, 0)),
                  pl.BlockSpec((1, d), lambda i: (0, 0)),
                  pl.BlockSpec(memory_space=pl.ANY)],
        out_specs=pl.BlockSpec((tm, d), lambda i: (i, 0)),
        out_shape=jax.ShapeDtypeStruct((n, d), F32),
        scratch_shapes=[pltpu.VMEM((TOP_K, tm, d), F32), pltpu.SemaphoreType.DMA(())],
        compiler_params=pltpu.CompilerParams(dimension_semantics=("arbitrary",),
                                             vmem_limit_bytes=VMEM_LIMIT),
        name="combine",
    )(dest.reshape(n // tm, 1, tm * TOP_K), route, x1, g2, gpost, ypad)


def _layer(x, ctx, mod, g_pre_mix, g_post_mix, g_pre_ffn, g_post_ffn, w_in, a_sink, b_rpb,
           w_branch_a, w_branch_b, w_out, w_router, b_router, w_gate_up, b_gate_up, w_down, b_down):
    batch, seq, d = x.shape
    ctx_len = ctx.shape[1]
    n = batch * seq
    xt = x.reshape(n, d)
    row = lambda v: v.reshape(1, -1)

    mod_x = mod[:batch].reshape(batch, 1, 6, d)
    sh1, sc1, g1, sh2, sc2, g2 = (mod_x[:, :, k] for k in range(6))
    mod_c = mod[batch:batch + 1].reshape(1, 1, 6, d)
    csh1, csc1 = mod_c[:, :, 0], mod_c[:, :, 1]

    assert A_KV_HEADS == 2
    group = A_HEADS // A_KV_HEADS
    head_order = [h for g in range(group) for h in (g, group + g)]
    head_cols = [slice(h * HEAD_DIM, (h + 1) * HEAD_DIM) for h in head_order]
    w_qkv = jnp.concatenate([w_in[:, cs] for cs in head_cols] + [w_in[:, A_Q:QKV_COLS]], axis=1).astype(BF16)
    w_branch_a = jnp.concatenate([w_branch_a[cs] for cs in head_cols], axis=0)
    w_gate = w_in[:, QKV_COLS:].astype(BF16)
    qa, kva, qb, kb, vb = _inproj_call(xt, seq, sh1, sc1, row(g_pre_mix), w_qkv, _rope_tables(seq))
    kva_c, kb_c, vb_c = _inproj_call(ctx.reshape(batch * ctx_len, d), ctx_len, csh1, csc1,
                                     row(g_pre_mix), w_qkv, None)

    ya = _attn_a_call(a_sink.astype(F32), qa, kva, kva_c, batch, seq, ctx_len)
    rows = seq // GRID_W
    bias = _nbr_bias_tables(b_rpb, rows, min(B_WIN_ROWS, rows))
    yb = _attn_b_call(qb, kb, vb, kb_c, vb_c, bias, batch, seq, ctx_len)

    x1, h2, route, counts = _merge_call(
        xt, ya, yb, (sh1, sc1, g1, sh2, sc2), row(g_pre_mix), row(g_post_mix), row(g_pre_ffn),
        w_gate, w_branch_a.astype(BF16), w_branch_b.astype(BF16), w_out.astype(BF16),
        w_router, row(b_router), seq)

    counts = counts.reshape(N_EXPERTS).astype(jnp.int32)
    padded = ((counts + MOE_ROWS - 1) // MOE_ROWS) * MOE_ROWS
    pend = jnp.cumsum(padded)
    pstart = pend - padded
    top_idx = route[:, IDX_LANE:IDX_LANE + TOP_K].astype(jnp.int32)
    rank = route[:, RANK_LANE:RANK_LANE + TOP_K].astype(jnp.int32)
    dest = pstart[top_idx] + rank
    n_blocks = (n * TOP_K) // MOE_ROWS + N_EXPERTS
    n_used = pend[-1] // MOE_ROWS
    blk = jnp.arange(n_blocks, dtype=jnp.int32)
    block_expert = jnp.sum((pend[None, :] <= (blk * MOE_ROWS)[:, None]).astype(jnp.int32), axis=1)
    block_expert = jnp.minimum(block_expert, N_EXPERTS - 1)
    block_expert = jnp.where(blk < n_used, block_expert, block_expert[jnp.maximum(n_used - 1, 0)])

    xpad = _dispatch_call(dest, h2, jnp.zeros((n_blocks * MOE_ROWS, d), F32))
    ypad = _experts_call(block_expert.astype(jnp.int32), n_used.reshape(1).astype(jnp.int32), xpad,
                         w_gate_up, b_gate_up, w_down, b_down)
    out = _combine_call(dest, route, x1, g2, row(g_post_ffn), ypad, seq)
    return out.reshape(batch, seq, d)


def kernel(x, c, ctx, c_ctx, w_mod, b_mod, g_pre_mix, g_post_mix, g_pre_ffn, g_post_ffn, w_in, a_sink, b_rpb,
           w_branch_a, w_branch_b, w_out, w_router, b_router, w_gate_up, b_gate_up, w_down, b_down):
    depth = w_mod.shape[0]
    batch = x.shape[0]
    assert depth == 1, "context-stream update between layers is not implemented"
    cc = jnp.zeros((16, D_MODEL), F32).at[:batch].set(c).at[batch].set(c_ctx)
    for l in range(depth):
        mod = _mod_call(cc, w_mod[l], b_mod[l])
        x = _layer(x, ctx, mod, g_pre_mix[l], g_post_mix[l], g_pre_ffn[l], g_post_ffn[l], w_in[l], a_sink[l],
                   b_rpb[l], w_branch_a[l], w_branch_b[l], w_out[l], w_router[l], b_router[l],
                   w_gate_up[l], b_gate_up[l], w_down[l], b_down[l])
    return x
```
